```python
import jax
import jax.numpy as jnp
from jax import lax
import numpy as np

D_MODEL = 1024
BATCH = 8
SEQ = 8192
DEPTH = 2

HEAD_DIM = 64
A_Q_HEADS = 8
A_KV_HEADS = 2
A_GROUP = A_Q_HEADS // A_KV_HEADS
WINDOW = 128
A_BLOCK = 128
B_HEADS = 8
B_BLOCK = 128
C_GROUPS = 8
C_GROUP_DIM = 64
C_CHUNK = 128
N_BRANCH = 3
A_WIDTH = A_Q_HEADS * HEAD_DIM
A_KV_WIDTH = A_KV_HEADS * HEAD_DIM
B_WIDTH = B_HEADS * HEAD_DIM
C_WIDTH = C_GROUPS * C_GROUP_DIM
GATE_WIDTH = N_BRANCH * D_MODEL
IN_WIDTH = A_WIDTH + 2 * A_KV_WIDTH + 3 * B_WIDTH + 2 * C_WIDTH + GATE_WIDTH
OFF_A_K = A_WIDTH
OFF_A_V = OFF_A_K + A_KV_WIDTH
OFF_B_Q = OFF_A_V + A_KV_WIDTH
OFF_B_K = OFF_B_Q + B_WIDTH
OFF_B_V = OFF_B_K + B_WIDTH
OFF_C_U = OFF_B_V + B_WIDTH
OFF_C_V = OFF_C_U + C_WIDTH
OFF_GATE = OFF_C_V + C_WIDTH
IN_SPLITS = (OFF_A_K, OFF_A_V, OFF_B_Q, OFF_B_K, OFF_B_V, OFF_C_U, OFF_C_V, OFF_GATE)
PEER_HEADS = 8
PEER_N_KEYS = 128
PEER_N_EXPERTS = PEER_N_KEYS * PEER_N_KEYS
PEER_TOPK = 16
PEER_HALF = 128
PEER_QDIM = 2 * PEER_HALF
PEER_TOKENS = 128
NORM_EPS = 1e-6

kernel_name = "hybrid_swa_stickbreak_sgu_peer"


def rmsnorm(x, g):
    xf = x.astype(jnp.float32)
    y = xf * lax.rsqrt(jnp.mean(xf * xf, axis=-1, keepdims=True) + NORM_EPS)
    return (y * g.astype(jnp.float32)).astype(x.dtype)


def sliding_window_sink_attention(q, k, v, sinks):
    bsz, s_len, _, hd = q.shape
    nb = s_len // A_BLOCK
    qb = q.reshape(bsz, nb, A_BLOCK, A_KV_HEADS, A_GROUP, hd)

    def band(t):
        tp = jnp.pad(t, ((0, 0), (A_BLOCK, 0), (0, 0), (0, 0)))
        prev = tp[:, :s_len].reshape(bsz, nb, A_BLOCK, A_KV_HEADS, hd)
        cur = t.reshape(bsz, nb, A_BLOCK, A_KV_HEADS, hd)
        return jnp.concatenate([prev, cur], axis=2)

    kb = band(k)
    vb = band(v)
    scores = jnp.einsum('bnqhgd,bnkhd->bnhgqk', qb, kb).astype(jnp.float32) * (hd ** -0.5)
    a = jnp.arange(A_BLOCK)[:, None]
    j = jnp.arange(2 * A_BLOCK)[None, :]
    rel = a + A_BLOCK - j
    in_window = (rel >= 0) & (rel < WINDOW)
    blk = jnp.arange(nb)[:, None, None]
    key_pos = blk * A_BLOCK - A_BLOCK + j[None]
    mask = in_window[None] & (key_pos >= 0)
    scores = jnp.where(mask[None, :, None, None], scores, -jnp.inf)
    sink = sinks.astype(jnp.float32).reshape(A_KV_HEADS, A_GROUP)[None, None, :, :, None, None]
    m = jnp.maximum(jnp.max(scores, axis=-1, keepdims=True), sink)
    p = jnp.exp(scores - m)
    denom = jnp.sum(p, axis=-1, keepdims=True) + jnp.exp(sink - m)
    p = (p / denom).astype(vb.dtype)
    out = jnp.einsum('bnhgqk,bnkhd->bnqhgd', p, vb)
    return out.reshape(bsz, s_len, A_Q_HEADS * hd)


def stick_breaking_attention(q, k, v):
    bsz, s_len, h, hd = q.shape
    scale = hd ** -0.5
    outs = []
    for i in range(s_len // B_BLOCK):
        q0 = i * B_BLOCK
        kend = q0 + B_BLOCK
        z = jnp.einsum('bqhd,bkhd->bhqk', q[:, q0:kend], k[:, :kend]).astype(jnp.float32) * scale
        t_pos = q0 + jnp.arange(B_BLOCK)[:, None]
        s_pos = jnp.arange(kend)[None, :]
        causal = s_pos < t_pos
        log_keep = jnp.where(causal, jax.nn.log_sigmoid(-z), 0.0)
        after = lax.cumsum(log_keep, axis=3, reverse=True) - log_keep
        w = jnp.where(causal, jnp.exp(jax.nn.log_sigmoid(z) + after), 0.0)
        outs.append(jnp.einsum('bhqk,bkhd->bqhd', w.astype(v.dtype), v[:, :kend]))
    return jnp.concatenate(outs, axis=1).reshape(bsz, s_len, h * hd)


def chunked_spatial_gating(u, v, norm_g, w_s, b_s):
    bsz, s_len, _ = u.shape
    nc = s_len // C_CHUNK
    u = jax.nn.gelu(u)
    v = rmsnorm(jax.nn.gelu(v), norm_g)
    ur = u.reshape(bsz, nc, C_CHUNK, C_GROUPS, C_GROUP_DIM)
    vr = v.reshape(bsz, nc, C_CHUNK, C_GROUPS, C_GROUP_DIM)
    tril = jnp.tril(jnp.ones((C_CHUNK, C_CHUNK), dtype=w_s.dtype))
    w = w_s * tril[None]
    mixed = jnp.einsum('gts,bnsgd->bntgd', w, vr) + jnp.transpose(b_s)[None, None, :, :, None]
    return (ur * mixed).reshape(bsz, s_len, C_WIDTH)


def peer_ffn(h, wq, k1, k2, eu, ev):
    bsz, s_len, d = h.shape
    hc = h.reshape((bsz * s_len) // PEER_TOKENS, PEER_TOKENS, d)

    def chunk(ht):
        q = (ht @ wq).reshape(PEER_TOKENS, PEER_HEADS, 2, PEER_HALF)
        s1 = jnp.einsum('thd,kd->thk', q[:, :, 0], k1).astype(jnp.float32)
        s2 = jnp.einsum('thd,kd->thk', q[:, :, 1], k2).astype(jnp.float32)
        v1, i1 = lax.top_k(s1, PEER_TOPK)
        v2, i2 = lax.top_k(s2, PEER_TOPK)
        cand = (v1[..., :, None] + v2[..., None, :]).reshape(PEER_TOKENS, PEER_HEADS, PEER_TOPK * PEER_TOPK)
        cid = (i1[..., :, None] * PEER_N_KEYS + i2[..., None, :]).reshape(PEER_TOKENS, PEER_HEADS, PEER_TOPK * PEER_TOPK)
        top, pos = lax.top_k(cand, PEER_TOPK)
        eid = jnp.take_along_axis(cid, pos, axis=-1)
        g = jax.nn.softmax(top, axis=-1)
        act = jnp.einsum('thkd,td->thk', eu[eid], ht).astype(jnp.float32)
        coef = (g * jax.nn.gelu(act)).astype(ht.dtype)
        return jnp.einsum('thk,thkd->td', coef, ev[eid])

    return lax.map(chunk, hc).reshape(bsz, s_len, d)


def setup_inputs(seed: int = 0) -> dict:
    key = jax.random.key(seed)
    ks = jax.random.split(key, 20)

    def nrm(k, shape, scale):
        return jax.random.normal(k, shape, jnp.float32) * scale

    L = DEPTH
    return {
        'x': nrm(ks[0], (BATCH, SEQ, D_MODEL), 1.0),
        'attn_norm_g': 1.0 + nrm(ks[1], (L, D_MODEL), 0.02),
        'w_in': nrm(ks[2], (L, D_MODEL, IN_WIDTH), D_MODEL ** -0.5),
        'a_q_norm_g': 1.0 + nrm(ks[3], (L, HEAD_DIM), 0.02),
        'a_k_norm_g': 1.0 + nrm(ks[4], (L, HEAD_DIM), 0.02),
        'a_sinks': nrm(ks[5], (L, A_Q_HEADS), 0.5),
        'c_norm_g': 1.0 + nrm(ks[6], (L, C_WIDTH), 0.02),
        'c_w_s': nrm(ks[7], (L, C_GROUPS, C_CHUNK, C_CHUNK), C_CHUNK ** -0.5),
        'c_b_s': 1.0 + nrm(ks[8], (L, C_GROUPS, C_CHUNK), 0.1),
        'w_branch_a': nrm(ks[9], (L, A_WIDTH, D_MODEL), A_WIDTH ** -0.5),
        'w_branch_b': nrm(ks[10], (L, B_WIDTH, D_MODEL), B_WIDTH ** -0.5),
        'w_branch_c': nrm(ks[11], (L, C_WIDTH, D_MODEL), C_WIDTH ** -0.5),
        'w_out': nrm(ks[12], (L, D_MODEL, D_MODEL), D_MODEL ** -0.5),
        'ffn_norm_g': 1.0 + nrm(ks[13], (L, D_MODEL), 0.02),
        'peer_w_q': nrm(ks[14], (L, D_MODEL, PEER_HEADS * PEER_QDIM), D_MODEL ** -0.5),
        'peer_k1': nrm(ks[15], (L, PEER_N_KEYS, PEER_HALF), PEER_HALF ** -0.5),
        'peer_k2': nrm(ks[16], (L, PEER_N_KEYS, PEER_HALF), PEER_HALF ** -0.5),
        'peer_u': nrm(ks[17], (L, PEER_N_EXPERTS, D_MODEL), D_MODEL ** -0.5),
        'peer_v': nrm(ks[18], (L, PEER_N_EXPERTS, D_MODEL), PEER_HEADS ** -0.5),
    }


def reference(x, attn_norm_g, w_in, a_q_norm_g, a_k_norm_g, a_sinks, c_norm_g, c_w_s, c_b_s,
              w_branch_a, w_branch_b, w_branch_c, w_out, ffn_norm_g, peer_w_q, peer_k1, peer_k2,
              peer_u, peer_v):
    bsz, s_len, _ = x.shape
    for l in range(DEPTH):
        h = rmsnorm(x, attn_norm_g[l])
        proj = h @ w_in[l]
        aq, ak, av, bq, bk, bv, cu, cv, gates = jnp.split(proj, IN_SPLITS, axis=-1)
        aq = rmsnorm(aq.reshape(bsz, s_len, A_Q_HEADS, HEAD_DIM), a_q_norm_g[l])
        ak = rmsnorm(ak.reshape(bsz, s_len, A_KV_HEADS, HEAD_DIM), a_k_norm_g[l])
        av = av.reshape(bsz, s_len, A_KV_HEADS, HEAD_DIM)
        ya = sliding_window_sink_attention(aq, ak, av, a_sinks[l])
        yb = stick_breaking_attention(bq.reshape(bsz, s_len, B_HEADS, HEAD_DIM),
                                      bk.reshape(bsz, s_len, B_HEADS, HEAD_DIM),
                                      bv.reshape(bsz, s_len, B_HEADS, HEAD_DIM))
        yc = chunked_spatial_gating(cu, cv, c_norm_g[l], c_w_s[l], c_b_s[l])
        g = jax.nn.sigmoid(gates.astype(jnp.float32)).reshape(bsz, s_len, N_BRANCH, D_MODEL)
        merged = (g[:, :, 0] * (ya @ w_branch_a[l]).astype(jnp.float32)
                  + g[:, :, 1] * (yb @ w_branch_b[l]).astype(jnp.float32)
                  + g[:, :, 2] * (yc @ w_branch_c[l]).astype(jnp.float32)).astype(x.dtype)
        x = x + merged @ w_out[l]
        h2 = rmsnorm(x, ffn_norm_g[l])
        x = x + peer_ffn(h2, peer_w_q[l], peer_k1[l], peer_k2[l], peer_u[l], peer_v[l])
    return x
```

```python
import functools

import jax
import jax.numpy as jnp
from jax import lax
from jax.experimental import pallas as pl
from jax.experimental.pallas import tpu as pltpu

F32 = jnp.float32
BF16 = jnp.bfloat16

D_MODEL = 1024
HEAD_DIM = 64
A_Q_HEADS = 8
A_KV_HEADS = 2
A_GROUP = A_Q_HEADS // A_KV_HEADS
BLOCK = 128
B_HEADS = 8
C_GROUPS = 8
A_WIDTH = A_Q_HEADS * HEAD_DIM
A_KV_WIDTH = A_KV_HEADS * HEAD_DIM
B_WIDTH = B_HEADS * HEAD_DIM
C_WIDTH = C_GROUPS * HEAD_DIM
PA_WIDTH = A_WIDTH + 2 * A_KV_WIDTH
PB_WIDTH = 3 * B_WIDTH
PC_WIDTH = 2 * C_WIDTH
GATE_WIDTH = 3 * D_MODEL
OFF_B = PA_WIDTH
OFF_C = OFF_B + PB_WIDTH
OFF_G = OFF_C + PC_WIDTH
IN_WIDTH = OFF_G + GATE_WIDTH
PEER_HEADS = 8
PEER_KEYS = 128
PEER_TOPK = 16
PEER_HALF = 128
PEER_QW = PEER_HEADS * 2 * PEER_HALF
PEER_EXPERTS = PEER_KEYS * PEER_KEYS
NORM_EPS = 1e-6
ATT_SCALE = HEAD_DIM ** -0.5

VMEM_LIMIT_BYTES = 56 * 1024 * 1024

SB_LOG_CUTOFF = -100.0

GELU_C = 0.7978845608028654


def _gelu(x):
    return 0.5 * x * (1.0 + jnp.tanh(GELU_C * (x + 0.044715 * (x * x * x))))


def _rms(x, g):
    return x * lax.rsqrt(jnp.mean(x * x, axis=-1, keepdims=True) + NORM_EPS) * g


def _dot(a, b):
    return jnp.dot(a, b, preferred_element_type=F32)


def _dot_nt(a, b):
    return lax.dot_general(a, b, (((1,), (1,)), ((), ())), preferred_element_type=F32)


def _params(*sem):
    return pltpu.CompilerParams(dimension_semantics=sem, vmem_limit_bytes=VMEM_LIMIT_BYTES)


IN_TM = 256


def _in_proj_kernel(x_ref, g_ref, w_ref, pa_ref, pb_ref, pc_ref, pg_ref):
    h = _rms(x_ref[...], g_ref[...]).astype(BF16)
    pa_ref[...] = _dot(h, w_ref[:, 0:OFF_B])
    pb_ref[...] = _dot(h, w_ref[:, OFF_B:OFF_C]).astype(BF16)
    pc_ref[...] = _dot(h, w_ref[:, OFF_C:OFF_G])
    pg_ref[...] = _dot(h, w_ref[:, OFF_G:IN_WIDTH])


def _in_proj(x2, g, w_bf):
    n = x2.shape[0]
    row = lambda i: (i, 0)
    const = lambda i: (0, 0)
    return pl.pallas_call(
        _in_proj_kernel,
        grid=(n // IN_TM,),
        in_specs=[
            pl.BlockSpec((IN_TM, D_MODEL), row),
            pl.BlockSpec((1, D_MODEL), const),
            pl.BlockSpec((D_MODEL, IN_WIDTH), const, pipeline_mode=pl.Buffered(1)),
        ],
        out_specs=[
            pl.BlockSpec((IN_TM, PA_WIDTH), row),
            pl.BlockSpec((IN_TM, PB_WIDTH), row),
            pl.BlockSpec((IN_TM, PC_WIDTH), row),
            pl.BlockSpec((IN_TM, GATE_WIDTH), row),
        ],
        out_shape=[
            jax.ShapeDtypeStruct((n, PA_WIDTH), F32),
            jax.ShapeDtypeStruct((n, PB_WIDTH), BF16),
            jax.ShapeDtypeStruct((n, PC_WIDTH), F32),
            jax.ShapeDtypeStruct((n, GATE_WIDTH), F32),
        ],
        compiler_params=_params("parallel"),
        name="in_proj",
    )(x2, g, w_bf)


SWA_ROWS = 512


def _swa_kernel(sink_ref, cur_ref, halo_ref, qg_ref, kg_ref, o_ref):
    first = pl.program_id(1) == 0
    qg = qg_ref[...]
    kg = kg_ref[...]
    qi = lax.broadcasted_iota(jnp.int32, (BLOCK, 2 * BLOCK), 0)
    kj = lax.broadcasted_iota(jnp.int32, (BLOCK, 2 * BLOCK), 1)
    band = (kj > qi) & (kj <= qi + BLOCK)
    for blk in range(SWA_ROWS // BLOCK):
        r0 = blk * BLOCK
        cur = cur_ref[r0:r0 + BLOCK, :]
        prev = halo_ref[...] if blk == 0 else cur_ref[r0 - BLOCK:r0, :]
        mask = band & ((kj >= BLOCK) | jnp.logical_not(first)) if blk == 0 else band
        for hk in range(A_KV_HEADS):
            kc = A_WIDTH + hk * HEAD_DIM
            vc = A_WIDTH + A_KV_WIDTH + hk * HEAD_DIM
            k = jnp.concatenate([prev[:, kc:kc + HEAD_DIM], cur[:, kc:kc + HEAD_DIM]], axis=0)
            v = jnp.concatenate([prev[:, vc:vc + HEAD_DIM], cur[:, vc:vc + HEAD_DIM]], axis=0)
            k = _rms(k, kg).astype(BF16)
            v = v.astype(BF16)
            for g in range(A_GROUP):
                hq = hk * A_GROUP + g
                q = _rms(cur[:, hq * HEAD_DIM:(hq + 1) * HEAD_DIM], qg).astype(BF16)
                s = _dot_nt(q, k) * ATT_SCALE
                s = jnp.where(mask, s, -jnp.inf)
                sink = sink_ref[hq]
                m = jnp.maximum(jnp.max(s, axis=-1, keepdims=True), sink)
                p = jnp.exp(s - m)
                denom = jnp.sum(p, axis=-1, keepdims=True) + jnp.exp(sink - m)
                out = _dot((p / denom).astype(BF16), v)
                o_ref[r0:r0 + BLOCK, hq * HEAD_DIM:(hq + 1) * HEAD_DIM] = out.astype(BF16)


def _swa(pa3, sinks, qg, kg):
    b, s, _ = pa3.shape
    per = SWA_ROWS // BLOCK
    return pl.pallas_call(
        _swa_kernel,
        grid=(b, s // SWA_ROWS),
        in_specs=[
            pl.BlockSpec(memory_space=pltpu.SMEM),
            pl.BlockSpec((None, SWA_ROWS, PA_WIDTH), lambda i, r: (i, r, 0)),
            pl.BlockSpec((None, BLOCK, PA_WIDTH), lambda i, r: (i, jnp.maximum(r * per - 1, 0), 0)),
            pl.BlockSpec((1, HEAD_DIM), lambda i, r: (0, 0)),
            pl.BlockSpec((1, HEAD_DIM), lambda i, r: (0, 0)),
        ],
        out_specs=pl.BlockSpec((None, SWA_ROWS, A_WIDTH), lambda i, r: (i, r, 0)),
        out_shape=jax.ShapeDtypeStruct((b, s, A_WIDTH), BF16),
        compiler_params=_params("parallel", "parallel"),
        name="swa",
    )(sinks, pa3, pa3, qg, kg)


def _sb_kernel(q_ref, k_ref, v_ref, o_ref):
    i = pl.program_id(2)
    q2 = q_ref[...]
    lane = lax.broadcasted_iota(jnp.int32, (BLOCK, BLOCK), 1)
    rowi = lax.broadcasted_iota(jnp.int32, (BLOCK, BLOCK), 0)
    low = lane < HEAD_DIM
    zero = jnp.zeros_like(q2)
    q_heads = (jnp.where(low, q2, zero), jnp.where(low, zero, q2))
    causal = lane < rowi
    later = (rowi > lane).astype(BF16)
    ones = jnp.ones((BLOCK, BLOCK), BF16)

    def tile(j, carry, diag):
        r0, r1, acc = carry
        off = pl.multiple_of(j * BLOCK, BLOCK)
        kj = k_ref[pl.ds(off, BLOCK), :]
        vj = v_ref[pl.ds(off, BLOCK), :]
        new_r = []
        pv = []
        for h, r in enumerate((r0, r1)):
            z = _dot_nt(q_heads[h], kj) * ATT_SCALE
            sp = jnp.maximum(z, 0.0) + jnp.log1p(jnp.exp(-jnp.abs(z)))
            lk = -sp
            if diag:
                lk = jnp.where(causal, lk, 0.0)
            hi = lk.astype(BF16)
            lo = (lk - hi.astype(F32)).astype(BF16)
            after = _dot(hi, later) + _dot(lo, later)
            rowsum = _dot(hi, ones) + _dot(lo, ones)
            w = jnp.exp((z - sp) + after + r)
            if diag:
                w = jnp.where(causal, w, 0.0)
            pv.append(_dot(w.astype(BF16), vj))
            new_r.append(r + rowsum)
        return new_r[0], new_r[1], acc + jnp.where(low, pv[0], pv[1])

    zeros = jnp.zeros((BLOCK, BLOCK), F32)
    carry = tile(i, (zeros, zeros, zeros), True)

    def cond(c):
        j, r0, r1, _ = c
        return jnp.logical_and(j >= 0, jnp.max(jnp.maximum(r0, r1)) > SB_LOG_CUTOFF)

    def body(c):
        j = c[0]
        return (j - 1,) + tile(j, c[1:], False)

    out = lax.while_loop(cond, body, (i - 1,) + carry)
    o_ref[...] = out[3].astype(BF16)


def _stickbreak(pb3):
    b, s, _ = pb3.shape
    pairs = B_WIDTH // BLOCK
    return pl.pallas_call(
        _sb_kernel,
        grid=(b, pairs, s // BLOCK),
        in_specs=[
            pl.BlockSpec((None, BLOCK, BLOCK), lambda bi, hp, i: (bi, i, hp)),
            pl.BlockSpec((None, s, BLOCK), lambda bi, hp, i: (bi, 0, pairs + hp)),
            pl.BlockSpec((None, s, BLOCK), lambda bi, hp, i: (bi, 0, 2 * pairs + hp)),
        ],
        out_specs=pl.BlockSpec((None, BLOCK, BLOCK), lambda bi, hp, i: (bi, i, hp)),
        out_shape=jax.ShapeDtypeStruct((b, s, B_WIDTH), BF16),
        compiler_params=_params("parallel", "parallel", "arbitrary"),
        name="stickbreak",
    )(pb3, pb3, pb3)


SGU_ROWS = 512


def _sgu_kernel(pc_ref, g_ref, w_ref, b_ref, o_ref):
    ti = lax.broadcasted_iota(jnp.int32, (BLOCK, BLOCK), 0)
    si = lax.broadcasted_iota(jnp.int32, (BLOCK, BLOCK), 1)
    low = si < HEAD_DIM
    w = [jnp.where(si <= ti, w_ref[g], 0.0).astype(BF16) for g in range(C_GROUPS)]
    for c in range(SGU_ROWS // BLOCK):
        rows = slice(c * BLOCK, (c + 1) * BLOCK)
        u = _gelu(pc_ref[rows, 0:C_WIDTH])
        v = _rms(_gelu(pc_ref[rows, C_WIDTH:PC_WIDTH]), g_ref[...]).astype(BF16)
        for p in range(C_WIDTH // BLOCK):
            cols = slice(p * BLOCK, (p + 1) * BLOCK)
            vp = v[:, cols]
            mixed = jnp.where(low, _dot(w[2 * p], vp), _dot(w[2 * p + 1], vp)) + b_ref[:, cols]
            o_ref[rows, cols] = (u[:, cols] * mixed).astype(BF16)


def _sgu(pc, g, w_s, bias):
    n = pc.shape[0]
    return pl.pallas_call(
        _sgu_kernel,
        grid=(n // SGU_ROWS,),
        in_specs=[
            pl.BlockSpec((SGU_ROWS, PC_WIDTH), lambda i: (i, 0)),
            pl.BlockSpec((1, C_WIDTH), lambda i: (0, 0)),
            pl.BlockSpec((C_GROUPS, BLOCK, BLOCK), lambda i: (0, 0, 0)),
            pl.BlockSpec((BLOCK, C_WIDTH), lambda i: (0, 0)),
        ],
        out_specs=pl.BlockSpec((SGU_ROWS, C_WIDTH), lambda i: (i, 0)),
        out_shape=jax.ShapeDtypeStruct((n, C_WIDTH), BF16),
        compiler_params=_params("parallel"),
        name="sgu",
    )(pc, g, w_s, bias)


MERGE_TM = 512


def _merge_kernel(x_ref, ya_ref, yb_ref, yc_ref, g_ref, wa_ref, wb_ref, wc_ref, wo_ref, o_ref):
    merged = jax.nn.sigmoid(g_ref[:, 0:D_MODEL]) * _dot(ya_ref[...], wa_ref[...])
    merged += jax.nn.sigmoid(g_ref[:, D_MODEL:2 * D_MODEL]) * _dot(yb_ref[...], wb_ref[...])
    merged += jax.nn.sigmoid(g_ref[:, 2 * D_MODEL:3 * D_MODEL]) * _dot(yc_ref[...], wc_ref[...])
    o_ref[...] = x_ref[...] + _dot(merged.astype(BF16), wo_ref[...])


def _merge_out(x2, ya, yb, yc, gates, wa, wb, wc, wo):
    n = x2.shape[0]
    row = lambda i: (i, 0)
    const = lambda i: (0, 0)
    wspec = lambda k: pl.BlockSpec((k, D_MODEL), const, pipeline_mode=pl.Buffered(1))
    return pl.pallas_call(
        _merge_kernel,
        grid=(n // MERGE_TM,),
        in_specs=[
            pl.BlockSpec((MERGE_TM, D_MODEL), row),
            pl.BlockSpec((MERGE_TM, A_WIDTH), row),
            pl.BlockSpec((MERGE_TM, B_WIDTH), row),
            pl.BlockSpec((MERGE_TM, C_WIDTH), row),
            pl.BlockSpec((MERGE_TM, GATE_WIDTH), row),
            wspec(A_WIDTH), wspec(B_WIDTH), wspec(C_WIDTH), wspec(D_MODEL),
        ],
        out_specs=pl.BlockSpec((MERGE_TM, D_MODEL), row),
        out_shape=jax.ShapeDtypeStruct((n, D_MODEL), F32),
        compiler_params=_params("parallel"),
        name="merge_out",
    )(x2, ya, yb, yc, gates, wa, wb, wc, wo)


ROUTE_TT = 256
PEER_PAIRS = tuple((i, j) for i in range(PEER_TOPK) for j in range(PEER_TOPK // (i + 1)))
PEER_PAIR_ROWS = -(-len(PEER_PAIRS) // 8) * 8


def _top16(s):
    vals = []
    for _ in range(PEER_TOPK):
        m = jnp.max(s, axis=0, keepdims=True)
        vals.append(m)
        s = jnp.where(s == m, -jnp.inf, s)
    return vals


def _route_kernel(x_ref, g_ref, wq_ref, k1_ref, k2_ref,
                  h_ref, s1_ref, s2_ref, p1_ref, p2_ref, tau_ref, q_scr):
    h2 = _rms(x_ref[...], g_ref[...]).astype(BF16)
    h_ref[...] = h2
    q_scr[...] = _dot_nt(wq_ref[...], h2)
    k1 = k1_ref[...]
    k2 = k2_ref[...]
    neg = jnp.full((PEER_PAIR_ROWS - len(PEER_PAIRS), ROUTE_TT), -jnp.inf, F32)

    for h in range(PEER_HEADS):
        base = h * 2 * PEER_HALF
        s1 = _dot(k1, q_scr[base:base + PEER_HALF, :].astype(BF16))
        s2 = _dot(k2, q_scr[base + PEER_HALF:base + 2 * PEER_HALF, :].astype(BF16))
        v1 = _top16(s1)
        v2 = _top16(s2)
        cand = jnp.concatenate([v1[i] + v2[j] for i, j in PEER_PAIRS] + [neg], axis=0)
        top = v1[0] + v2[0]
        rest = cand
        for _ in range(PEER_TOPK - 1):
            rest = jnp.where(rest == jnp.max(rest, axis=0, keepdims=True), -jnp.inf, rest)
        tau = jnp.max(rest, axis=0, keepdims=True)
        z = jnp.sum(jnp.where(cand >= tau, jnp.exp(cand - top), 0.0), axis=0, keepdims=True)
        s1_ref[h] = s1
        s2_ref[h] = s2
        p1_ref[h] = jnp.exp(s1 - v1[0]) / z
        p2_ref[h] = jnp.exp(s2 - v2[0])
        tau_ref[h:h + 1, :] = tau


def _peer_route(x2, g, wq_t, k1, k2):
    n = x2.shape[0]
    row = lambda i: (i, 0)
    const = lambda i: (0, 0)
    tok3 = lambda i: (0, 0, i)
    dense = jax.ShapeDtypeStruct((PEER_HEADS, PEER_KEYS, n), F32)
    dense_spec = pl.BlockSpec((PEER_HEADS, PEER_KEYS, ROUTE_TT), tok3)
    return pl.pallas_call(
        _route_kernel,
        grid=(n // ROUTE_TT,),
        in_specs=[
            pl.BlockSpec((ROUTE_TT, D_MODEL), row),
            pl.BlockSpec((1, D_MODEL), const),
            pl.BlockSpec((PEER_QW, D_MODEL), const, pipeline_mode=pl.Buffered(1)),
            pl.BlockSpec((PEER_KEYS, PEER_HALF), const),
            pl.BlockSpec((PEER_KEYS, PEER_HALF), const),
        ],
        out_specs=[
            pl.BlockSpec((ROUTE_TT, D_MODEL), row),
            dense_spec, dense_spec, dense_spec, dense_spec,
            pl.BlockSpec((PEER_HEADS, ROUTE_TT), lambda i: (0, i)),
        ],
        out_shape=[
            jax.ShapeDtypeStruct((n, D_MODEL), BF16),
            dense, dense, dense, dense,
            jax.ShapeDtypeStruct((PEER_HEADS, n), F32),
        ],
        scratch_shapes=[pltpu.VMEM((PEER_QW, ROUTE_TT), F32)],
        compiler_params=_params("parallel"),
        name="peer_route",
    )(x2, g, wq_t, k1, k2)


DENSE_TT = 512
DENSE_ET = 1024
DENSE_A = DENSE_ET // PEER_KEYS


def _dense_kernel(x_ref, h_ref, u_ref, vt_ref, s1_ref, p1_ref, s2_ref, p2_ref, tau_ref,
                  o_ref, acc_ref, act_ref, coef_ref):
    e = pl.program_id(1)

    @pl.when(e == 0)
    def _():
        acc_ref[...] = jnp.zeros_like(acc_ref)

    act_ref[...] = _dot_nt(u_ref[...], h_ref[...])

    for a in range(DENSE_A):
        rows = slice(a * PEER_KEYS, (a + 1) * PEER_KEYS)
        for lg in range(DENSE_TT // BLOCK):
            cols = slice(lg * BLOCK, (lg + 1) * BLOCK)
            w = jnp.zeros((PEER_KEYS, BLOCK), F32)
            for h in range(PEER_HEADS):
                score = s2_ref[h, :, cols] + s1_ref[h, a:a + 1, cols]
                weight = p2_ref[h, :, cols] * p1_ref[h, a:a + 1, cols]
                w = w + jnp.where(score >= tau_ref[h:h + 1, cols], weight, 0.0)
            coef_ref[rows, cols] = (w * _gelu(act_ref[rows, cols])).astype(BF16)
    acc_ref[...] += _dot(vt_ref[...], coef_ref[...])

    @pl.when(e == pl.num_programs(1) - 1)
    def _():
        o_ref[...] = x_ref[...] + acc_ref[...].T


def _peer_dense(x2, h2, u_bf, vt_bf, s1, p1, s2, p2, tau):
    n = x2.shape[0]
    tok = lambda i, e: (i, 0)
    sel_spec = pl.BlockSpec((PEER_HEADS, DENSE_A, DENSE_TT), lambda i, e: (0, e, i))
    all_spec = pl.BlockSpec((PEER_HEADS, PEER_KEYS, DENSE_TT), lambda i, e: (0, 0, i))
    return pl.pallas_call(
        _dense_kernel,
        grid=(n // DENSE_TT, PEER_EXPERTS // DENSE_ET),
        in_specs=[
            pl.BlockSpec((DENSE_TT, D_MODEL), tok),
            pl.BlockSpec((DENSE_TT, D_MODEL), tok),
            pl.BlockSpec((DENSE_ET, D_MODEL), lambda i, e: (e, 0)),
            pl.BlockSpec((D_MODEL, DENSE_ET), lambda i, e: (0, e)),
            sel_spec, sel_spec, all_spec, all_spec,
            pl.BlockSpec((PEER_HEADS, DENSE_TT), lambda i, e: (0, i)),
        ],
        out_specs=pl.BlockSpec((DENSE_TT, D_MODEL), tok),
        out_shape=jax.ShapeDtypeStruct((n, D_MODEL), F32),
        scratch_shapes=[
            pltpu.VMEM((D_MODEL, DENSE_TT), F32),
            pltpu.VMEM((DENSE_ET, DENSE_TT), F32),
            pltpu.VMEM((DENSE_ET, DENSE_TT), BF16),
        ],
        compiler_params=_params("parallel", "arbitrary"),
        name="peer_dense",
    )(x2, h2, u_bf, vt_bf, s1, p1, s2, p2, tau)


def kernel(x, attn_norm_g, w_in, a_q_norm_g, a_k_norm_g, a_sinks, c_norm_g, c_w_s, c_b_s,
           w_branch_a, w_branch_b, w_branch_c, w_out, ffn_norm_g, peer_w_q, peer_k1, peer_k2,
           peer_u, peer_v):
    bsz, s_len, d = x.shape
    n = bsz * s_len
    assert d == D_MODEL and s_len % SWA_ROWS == 0
    assert n % max(IN_TM, SGU_ROWS, MERGE_TM, ROUTE_TT, DENSE_TT) == 0
    x2 = x.reshape(n, d)
    for l in range(attn_norm_g.shape[0]):
        pa, pb, pc, gates = _in_proj(x2, attn_norm_g[l][None], w_in[l].astype(BF16))
        ya = _swa(pa.reshape(bsz, s_len, PA_WIDTH), a_sinks[l],
                  a_q_norm_g[l][None], a_k_norm_g[l][None])
        yb = _stickbreak(pb.reshape(bsz, s_len, PB_WIDTH))
        bias = jnp.repeat(c_b_s[l].T, HEAD_DIM, axis=1)
        yc = _sgu(pc, c_norm_g[l][None], c_w_s[l], bias)
        x2 = _merge_out(x2, ya.reshape(n, A_WIDTH), yb.reshape(n, B_WIDTH), yc, gates,
                        w_branch_a[l].astype(BF16), w_branch_b[l].astype(BF16),
                        w_branch_c[l].astype(BF16), w_out[l].astype(BF16))
        h2, s1, s2, p1, p2, tau = _peer_route(
            x2, ffn_norm_g[l][None], peer_w_q[l].T.astype(BF16),
            peer_k1[l].astype(BF16), peer_k2[l].astype(BF16))
        x2 = _peer_dense(x2, h2, peer_u[l].astype(BF16), peer_v[l].T.astype(BF16),
                         s1, p1, s2, p2, tau)
    return x2.reshape(bsz, s_len, d)
```
